```python
import jax, jax.numpy as jnp
from jax import lax
import numpy as np

D_MODEL = 1024
BATCH = 16
SEQ = 2048
DEPTH = 4

GRID_W = 64
CTX_LEN = 256
N_HEADS = 16
N_KV_HEADS = 4
HEAD_DIM = 64
GROUP = N_HEADS // N_KV_HEADS
ROPE_THETA = 10000.0
ROPE_PAIRS_PER_AXIS = HEAD_DIM // 4
Q_BLOCK = 128
D_CONV = D_MODEL
CONV_K = 31
CONV_PAD = (CONV_K - 1) // 2
D_FF = 2816
N_EXPERTS = 8
TOP_K = 2
D_EXPERT = 3584
MOE_BLOCK = 128
N_BRANCHES = 2
N_MOD = 6
EPS = 1e-6

Q_W = N_HEADS * HEAD_DIM
KV_W = N_KV_HEADS * HEAD_DIM
OFF_K = Q_W
OFF_V = OFF_K + KV_W
OFF_CONV = OFF_V + KV_W
OFF_GATE = OFF_CONV + 2 * D_CONV
IN_W = OFF_GATE + N_BRANCHES * D_MODEL
N_DENSE = (DEPTH + 1) // 2
N_MOE = DEPTH // 2

kernel_name = "hybrid_gqa_conformer_moe_dit"


def rmsnorm(x, g):
    xf = x.astype(jnp.float32)
    y = xf * lax.rsqrt(jnp.mean(xf * xf, axis=-1, keepdims=True) + EPS)
    return (y * g.astype(jnp.float32)).astype(x.dtype)


def layernorm(x, g, b):
    xf = x.astype(jnp.float32)
    mu = jnp.mean(xf, axis=-1, keepdims=True)
    var = jnp.mean(jnp.square(xf - mu), axis=-1, keepdims=True)
    y = (xf - mu) * lax.rsqrt(var + EPS)
    return (y * g.astype(jnp.float32) + b.astype(jnp.float32)).astype(x.dtype)


def modulate(h, shift, scale):
    return h * (1.0 + scale) + shift


def axial_rope(seq_len):
    rows = seq_len // GRID_W
    row = jnp.repeat(jnp.arange(rows, dtype=jnp.float32), GRID_W)
    col = jnp.tile(jnp.arange(GRID_W, dtype=jnp.float32), rows)
    freqs = ROPE_THETA ** (-jnp.arange(ROPE_PAIRS_PER_AXIS, dtype=jnp.float32) / ROPE_PAIRS_PER_AXIS)
    ang = jnp.concatenate([row[:, None] * freqs, col[:, None] * freqs], axis=-1)
    return jnp.cos(ang)[:, None, :], jnp.sin(ang)[:, None, :]


def apply_rope(x, cos, sin):
    xf = x.astype(jnp.float32)
    half = HEAD_DIM // 2
    x1, x2 = xf[..., :half], xf[..., half:]
    return jnp.concatenate([x1 * cos - x2 * sin, x1 * sin + x2 * cos], axis=-1).astype(x.dtype)


def gqa_block(q, k, v):
    s = jnp.einsum('bqkgd,blkd->bkgql', q, k, preferred_element_type=jnp.float32) * (HEAD_DIM ** -0.5)
    p = jax.nn.softmax(s, axis=-1).astype(v.dtype)
    return jnp.einsum('bkgql,blkd->bqkgd', p, v)


def latent_attention(q, k, v):
    b, s = q.shape[:2]
    nb = s // Q_BLOCK
    qb = q.reshape(b, nb, Q_BLOCK, N_KV_HEADS, GROUP, HEAD_DIM).swapaxes(0, 1)
    o = lax.map(lambda qblk: gqa_block(qblk, k, v), qb)
    return o.swapaxes(0, 1).reshape(b, s, Q_W)


def conformer_conv(u, dw, db, lg, lb, wo):
    a, g = jnp.split(u, 2, axis=-1)
    z = a * jax.nn.sigmoid(g)
    z = lax.conv_general_dilated(z, dw[:, None, :].astype(z.dtype), window_strides=(1,),
                                 padding=[(CONV_PAD, CONV_PAD)],
                                 dimension_numbers=('NWC', 'WIO', 'NWC'),
                                 feature_group_count=D_CONV) + db
    z = jax.nn.silu(layernorm(z, lg, lb))
    return z @ wo


def merge_branches(p, attn_o, conv_o, w_attn_o, w_out):
    ga, gc = jnp.split(jax.nn.sigmoid(p[..., OFF_GATE:]), N_BRANCHES, axis=-1)
    return (ga * (attn_o @ w_attn_o) + gc * conv_o) @ w_out


def swiglu(h, wg, wu, wd):
    return (jax.nn.silu(h @ wg) * (h @ wu)) @ wd


def moe_swiglu(h, w_router, b_router, w_g, w_u, w_d):
    n, d = h.shape
    logits = jnp.einsum('nd,de->ne', h, w_router, preferred_element_type=jnp.float32) + b_router.astype(jnp.float32)
    top_logit, top_idx = lax.top_k(logits, TOP_K)
    gates = jax.nn.softmax(top_logit, axis=-1)
    flat_e = top_idx.reshape(-1)
    flat_tok = jnp.repeat(jnp.arange(n, dtype=jnp.int32), TOP_K)
    flat_gate = gates.reshape(-1)
    order = jnp.argsort(flat_e, stable=True)
    e_sorted = flat_e[order]
    counts = jnp.bincount(flat_e, length=N_EXPERTS)
    starts = jnp.cumsum(counts) - counts
    pcounts = (counts + MOE_BLOCK - 1) // MOE_BLOCK * MOE_BLOCK
    pends = jnp.cumsum(pcounts)
    pstarts = pends - pcounts
    dest = pstarts[e_sorted] + jnp.arange(n * TOP_K, dtype=jnp.int32) - starts[e_sorted]
    n_blocks = -(-(n * TOP_K) // MOE_BLOCK) + N_EXPERTS
    total = n_blocks * MOE_BLOCK
    tok_buf = jnp.zeros((total,), jnp.int32).at[dest].set(flat_tok[order])
    gate_buf = jnp.zeros((total,), h.dtype).at[dest].set(flat_gate[order].astype(h.dtype))
    block_start = jnp.arange(n_blocks, dtype=jnp.int32) * MOE_BLOCK
    block_e = jnp.minimum(jnp.sum(block_start[:, None] >= pends[None, :], axis=1), N_EXPERTS - 1)

    def run_block(args):
        tok, e = args
        xb = h[tok]
        return (jax.nn.silu(xb @ w_g[e]) * (xb @ w_u[e])) @ w_d[e]

    out = lax.map(run_block, (tok_buf.reshape(n_blocks, MOE_BLOCK), block_e))
    out = out.reshape(total, d) * gate_buf[:, None]
    return jnp.zeros_like(h).at[tok_buf].add(out)


def setup_inputs(seed: int = 0) -> dict:
    key = jax.random.key(seed)
    ks = iter(jax.random.split(key, 40))

    def nrm(shape, scale):
        return jax.random.normal(next(ks), shape, jnp.float32) * scale

    def gain(shape):
        return 1.0 + nrm(shape, 0.02)

    D = D_MODEL
    return {
        "x": nrm((BATCH, SEQ, D), 1.0),
        "c": nrm((BATCH, D), 1.0),
        "ctx": nrm((BATCH, CTX_LEN, D), 1.0),
        "c_ctx": nrm((D,), 1.0),
        "w_ada": nrm((DEPTH, D, N_MOD * D), D ** -0.5),
        "b_ada": nrm((DEPTH, N_MOD * D), 0.02),
        "g_mix": gain((DEPTH, D)),
        "g_ffn": gain((DEPTH, D)),
        "w_in": nrm((DEPTH, D, IN_W), D ** -0.5),
        "q_norm": gain((DEPTH, HEAD_DIM)),
        "k_norm": gain((DEPTH, HEAD_DIM)),
        "w_attn_o": nrm((DEPTH, Q_W, D), Q_W ** -0.5),
        "conv_dw": nrm((DEPTH, CONV_K, D_CONV), CONV_K ** -0.5),
        "conv_b": nrm((DEPTH, D_CONV), 0.02),
        "conv_ln_g": gain((DEPTH, D_CONV)),
        "conv_ln_b": nrm((DEPTH, D_CONV), 0.02),
        "w_conv_o": nrm((DEPTH, D_CONV, D), D_CONV ** -0.5),
        "w_out": nrm((DEPTH, D, D), D ** -0.5),
        "w_ff_gate": nrm((N_DENSE, D, D_FF), D ** -0.5),
        "w_ff_up": nrm((N_DENSE, D, D_FF), D ** -0.5),
        "w_ff_down": nrm((N_DENSE, D_FF, D), D_FF ** -0.5),
        "w_router": nrm((N_MOE, D, N_EXPERTS), D ** -0.5),
        "b_router": nrm((N_MOE, N_EXPERTS), 0.01),
        "w_moe_gate": nrm((N_MOE, N_EXPERTS, D, D_EXPERT), D ** -0.5),
        "w_moe_up": nrm((N_MOE, N_EXPERTS, D, D_EXPERT), D ** -0.5),
        "w_moe_down": nrm((N_MOE, N_EXPERTS, D_EXPERT, D), D_EXPERT ** -0.5),
        "g_final": gain((D,)),
    }


def reference(x, c, ctx, c_ctx, w_ada, b_ada, g_mix, g_ffn, w_in, q_norm, k_norm, w_attn_o,
              conv_dw, conv_b, conv_ln_g, conv_ln_b, w_conv_o, w_out, w_ff_gate, w_ff_up,
              w_ff_down, w_router, b_router, w_moe_gate, w_moe_up, w_moe_down, g_final):
    b, s, d = x.shape
    cos, sin = axial_rope(s)
    xc = ctx
    for l in range(DEPTH):
        last = l == DEPTH - 1
        mod_x = (jax.nn.silu(c) @ w_ada[l] + b_ada[l])[:, None, :]
        mod_c = jax.nn.silu(c_ctx) @ w_ada[l] + b_ada[l]
        sh1, sc1, gt1, sh2, sc2, gt2 = jnp.split(mod_x, N_MOD, axis=-1)
        csh1, csc1, cgt1, csh2, csc2, cgt2 = jnp.split(mod_c, N_MOD, axis=-1)

        hx = modulate(rmsnorm(x, g_mix[l]), sh1, sc1)
        hc = modulate(rmsnorm(xc, g_mix[l]), csh1, csc1)
        w_l = w_in[l]
        px = hx @ w_l
        pc = hc @ (w_l[:, OFF_K:OFF_CONV] if last else w_l)
        off = OFF_K if last else 0

        qx = apply_rope(rmsnorm(px[..., :OFF_K].reshape(b, s, N_HEADS, HEAD_DIM), q_norm[l]), cos, sin)
        kx = apply_rope(rmsnorm(px[..., OFF_K:OFF_V].reshape(b, s, N_KV_HEADS, HEAD_DIM), k_norm[l]), cos, sin)
        vx = px[..., OFF_V:OFF_CONV].reshape(b, s, N_KV_HEADS, HEAD_DIM)
        kc = rmsnorm(pc[..., OFF_K - off:OFF_V - off].reshape(b, -1, N_KV_HEADS, HEAD_DIM), k_norm[l])
        vc = pc[..., OFF_V - off:OFF_CONV - off].reshape(b, -1, N_KV_HEADS, HEAD_DIM)

        k_all = jnp.concatenate([kx, kc], axis=1)
        v_all = jnp.concatenate([vx, vc], axis=1)
        attn_x = latent_attention(qx, k_all, v_all)
        conv_x = conformer_conv(px[..., OFF_CONV:OFF_GATE], conv_dw[l], conv_b[l],
                                conv_ln_g[l], conv_ln_b[l], w_conv_o[l])
        x = x + gt1 * merge_branches(px, attn_x, conv_x, w_attn_o[l], w_out[l])

        if not last:
            qc = rmsnorm(pc[..., :OFF_K].reshape(b, -1, N_HEADS, HEAD_DIM), q_norm[l])
            qc = qc.reshape(b, -1, N_KV_HEADS, GROUP, HEAD_DIM)
            attn_c = gqa_block(qc, kc, vc).reshape(b, -1, Q_W)
            conv_c = conformer_conv(pc[..., OFF_CONV:OFF_GATE], conv_dw[l], conv_b[l],
                                    conv_ln_g[l], conv_ln_b[l], w_conv_o[l])
            xc = xc + cgt1 * merge_branches(pc, attn_c, conv_c, w_attn_o[l], w_out[l])

        tokens = modulate(rmsnorm(x, g_ffn[l]), sh2, sc2).reshape(-1, d)
        if not last:
            hc2 = modulate(rmsnorm(xc, g_ffn[l]), csh2, csc2).reshape(-1, d)
            tokens = jnp.concatenate([tokens, hc2], axis=0)
        i = l // 2
        if l % 2 == 0:
            y = swiglu(tokens, w_ff_gate[i], w_ff_up[i], w_ff_down[i])
        else:
            y = moe_swiglu(tokens, w_router[i], b_router[i], w_moe_gate[i], w_moe_up[i], w_moe_down[i])
        n_lat = b * s
        x = x + gt2 * y[:n_lat].reshape(b, s, d)
        if not last:
            xc = xc + cgt2 * y[n_lat:].reshape(b, -1, d)
    return rmsnorm(x, g_final)
```

```python
import functools

import jax
import jax.numpy as jnp
from jax import lax
from jax.experimental import pallas as pl
from jax.experimental.pallas import tpu as pltpu

N_HEADS = 16
N_KV_HEADS = 4
HEAD_DIM = 64
GROUP = N_HEADS // N_KV_HEADS
GRID_W = 64
ROPE_THETA = 10000.0
CONV_K = 31
CONV_PAD = (CONV_K - 1) // 2
N_MOD = 6
EPS = 1e-6
LANES = 128
HALO = 16
EXPERT_LANES = 128
NEG_BIG = -3.0e38
VMEM_LIMIT = 56 * 1024 * 1024

f32 = jnp.float32
bf16 = jnp.bfloat16


def _sigmoid(x):
    return 1.0 / (1.0 + jnp.exp(-x))


def _dot(a, b):
    return jnp.dot(a, b, preferred_element_type=f32)


def _resident(shape):
    zeros = (0,) * len(shape)
    return pl.BlockSpec(shape, lambda *_: zeros, pipeline_mode=pl.Buffered(1))


def _params(sem, vmem=VMEM_LIMIT):
    return pltpu.CompilerParams(dimension_semantics=sem, vmem_limit_bytes=vmem)


def _ada_kernel(c_ref, w_ref, b_ref, o_ref):
    cv = c_ref[...]
    a = (cv * _sigmoid(cv)).astype(bf16)
    o_ref[0] = _dot(a, w_ref[0].astype(bf16)) + b_ref[0]


def _ada_call(cvec, w_ada, b_ada):
    depth, d, n = w_ada.shape
    r = cvec.shape[0]
    tn = 1536
    return pl.pallas_call(
        _ada_kernel,
        grid=(depth, n // tn),
        in_specs=[
            pl.BlockSpec((r, d), lambda l, t: (0, 0)),
            pl.BlockSpec((1, d, tn), lambda l, t: (l, 0, t)),
            pl.BlockSpec((1, 1, tn), lambda l, t: (l, 0, t)),
        ],
        out_specs=pl.BlockSpec((1, r, tn), lambda l, t: (l, 0, t)),
        out_shape=jax.ShapeDtypeStruct((depth, r, n), f32),
        compiler_params=_params(("arbitrary", "arbitrary")),
        name="ada_mod",
    )(cvec, w_ada, b_ada.reshape(depth, 1, n))


def _inproj_kernel(x_ref, mod_ref, g_ref, w_ref, qn_ref, kn_ref, cos_ref, sin_ref, seg_ref,
                   q_ref, kt_ref, v_ref, z_ref, gate_ref):
    d = x_ref.shape[-1]
    q_w = N_HEADS * HEAD_DIM
    kv_w = N_KV_HEADS * HEAD_DIM
    off_k, off_v, off_conv = q_w, q_w + kv_w, q_w + 2 * kv_w
    off_gate = off_conv + 2 * d

    x = x_ref[0]
    m = mod_ref[0]
    sh, sc = m[:, 0:d], m[:, d:2 * d]
    r = lax.rsqrt(jnp.mean(x * x, axis=-1, keepdims=True) + EPS)
    h = ((x * r * g_ref[...]) * (1.0 + sc) + sh).astype(bf16)

    def proj(lo, hi):
        return _dot(h, w_ref[:, lo:hi])

    seg = seg_ref[...]
    lane = lax.broadcasted_iota(jnp.int32, (1, LANES), 1)
    first_half = (lane % HEAD_DIM) < (HEAD_DIM // 2)
    cos = cos_ref[...]
    sin = sin_ref[...]

    def norm_rope(chunk, gain):
        c2 = chunk * chunk
        hi = c2.astype(bf16)
        lo = (c2 - hi.astype(f32)).astype(bf16)
        ss = _dot(hi, seg) + _dot(lo, seg)
        y = chunk * lax.rsqrt(ss * (1.0 / HEAD_DIM) + EPS) * gain
        rot = jnp.where(first_half, pltpu.roll(y, LANES - HEAD_DIM // 2, 1),
                        pltpu.roll(y, HEAD_DIM // 2, 1))
        return y * cos + rot * sin

    pq = proj(0, off_k)
    qn = qn_ref[...]
    scale = HEAD_DIM ** -0.5
    for c in range(q_w // LANES):
        sl = slice(c * LANES, (c + 1) * LANES)
        q_ref[0, :, sl] = (norm_rope(pq[:, sl], qn) * scale).astype(bf16)

    pk = proj(off_k, off_v)
    kn = kn_ref[...]
    kk = jnp.concatenate([norm_rope(pk[:, c * LANES:(c + 1) * LANES], kn)
                          for c in range(kv_w // LANES)], axis=1)
    kt_ref[0] = kk.T.astype(bf16)

    pv = proj(off_v, off_conv)
    for hh in range(N_KV_HEADS):
        v_ref[0, hh] = pv[:, hh * HEAD_DIM:(hh + 1) * HEAD_DIM].astype(bf16)

    a = proj(off_conv, off_conv + d)
    g = proj(off_conv + d, off_gate)
    z_ref[0] = (a * _sigmoid(g)).astype(bf16)
    gate_ref[0] = _sigmoid(proj(off_gate, off_gate + 2 * d)).astype(bf16)


def _inproj_call(x, mod, g_mix, w_in, qn, kn, cos_t, sin_t, seg, *, tm, n_lat):
    b, t, d = x.shape
    nt = t // tm
    in_w = w_in.shape[1]
    q_w, kv_w = N_HEADS * HEAD_DIM, N_KV_HEADS * HEAD_DIM
    nb = mod.shape[0] - 1
    tile = lambda bb, j: (bb, j, 0)
    return pl.pallas_call(
        _inproj_kernel,
        grid=(b, nt),
        in_specs=[
            pl.BlockSpec((1, tm, d), tile),
            pl.BlockSpec((1, 1, N_MOD * d), lambda bb, j: (jnp.where(j < n_lat, bb, nb), 0, 0)),
            _resident((1, d)),
            _resident((d, in_w)),
            _resident((1, LANES)),
            _resident((1, LANES)),
            pl.BlockSpec((tm, LANES), lambda bb, j: (j, 0)),
            pl.BlockSpec((tm, LANES), lambda bb, j: (j, 0)),
            _resident((LANES, LANES)),
        ],
        out_specs=[
            pl.BlockSpec((1, tm, q_w), tile),
            pl.BlockSpec((1, kv_w, tm), lambda bb, j: (bb, 0, j)),
            pl.BlockSpec((1, N_KV_HEADS, tm, HEAD_DIM), lambda bb, j: (bb, 0, j, 0)),
            pl.BlockSpec((1, tm, d), tile),
            pl.BlockSpec((1, tm, 2 * d), tile),
        ],
        out_shape=[
            jax.ShapeDtypeStruct((b, t, q_w), bf16),
            jax.ShapeDtypeStruct((b, kv_w, t), bf16),
            jax.ShapeDtypeStruct((b, N_KV_HEADS, t, HEAD_DIM), bf16),
            jax.ShapeDtypeStruct((b, t, d), bf16),
            jax.ShapeDtypeStruct((b, t, 2 * d), bf16),
        ],
        compiler_params=_params(("parallel", "parallel")),
        name="in_proj",
    )(x, mod, g_mix, w_in, qn, kn, cos_t, sin_t, seg)


def _attn_kernel(q_ref, kt_ref, v_ref, o_ref, *, n_lat, s_len):
    j = pl.program_id(2)
    t = kt_ref.shape[-1]

    def run(k0):
        kt = kt_ref[0, :, k0:t]
        v = v_ref[0, 0, k0:t, :]
        outs = []
        for g in range(GROUP):
            qg = q_ref[0, :, g * HEAD_DIM:(g + 1) * HEAD_DIM]
            s = _dot(qg, kt)
            mx = jnp.max(s, axis=-1, keepdims=True)
            p = jnp.exp(s - mx)
            den = jnp.sum(p, axis=-1, keepdims=True)
            outs.append(_dot(p.astype(bf16), v) / den)
        o_ref[0] = jnp.concatenate(outs, axis=1).astype(bf16)

    @pl.when(j < n_lat)
    def _():
        run(0)

    @pl.when(j >= n_lat)
    def _():
        run(s_len)


def _attn_call(q, kt, v, *, tm, n_lat, nq, s_len):
    b, t, q_w = q.shape
    gw = GROUP * HEAD_DIM
    return pl.pallas_call(
        functools.partial(_attn_kernel, n_lat=n_lat, s_len=s_len),
        grid=(b, N_KV_HEADS, nq),
        in_specs=[
            pl.BlockSpec((1, tm, gw), lambda bb, h, j: (bb, j, h)),
            pl.BlockSpec((1, HEAD_DIM, t), lambda bb, h, j: (bb, h, 0)),
            pl.BlockSpec((1, 1, t, HEAD_DIM), lambda bb, h, j: (bb, h, 0, 0)),
        ],
        out_specs=pl.BlockSpec((1, tm, gw), lambda bb, h, j: (bb, j, h)),
        out_shape=jax.ShapeDtypeStruct((b, nq * tm, q_w), bf16),
        compiler_params=_params(("parallel", "parallel", "parallel")),
        name="attention",
    )(q, kt, v)


def _conv_kernel(z_ref, w_ref, cb_ref, lg_ref, lb_ref, o_ref, win_ref, *, tm, n_lat, rc):
    j = pl.program_id(1)
    t, d = z_ref.shape[1], z_ref.shape[2]
    nt = t // tm

    row0 = pl.multiple_of(j * tm, tm)
    win_ref[HALO:HALO + tm] = z_ref[0, pl.ds(row0, tm), :].astype(f32)
    seg_start = jnp.logical_or(j == 0, j == n_lat)
    seg_end = jnp.logical_or(j == n_lat - 1, j == nt - 1)

    @pl.when(seg_start)
    def _():
        win_ref[0:HALO] = jnp.zeros((HALO, d), f32)

    @pl.when(jnp.logical_not(seg_start))
    def _():
        win_ref[0:HALO] = z_ref[0, pl.ds(pl.multiple_of(row0 - HALO, HALO), HALO), :].astype(f32)

    @pl.when(seg_end)
    def _():
        win_ref[HALO + tm:2 * HALO + tm] = jnp.zeros((HALO, d), f32)

    @pl.when(jnp.logical_not(seg_end))
    def _():
        win_ref[HALO + tm:2 * HALO + tm] = z_ref[0, pl.ds(pl.multiple_of(row0 + tm, HALO), HALO), :].astype(f32)

    lg = lg_ref[...]
    lb = lb_ref[...]
    shift = HALO - CONV_PAD
    for ci in range(tm // rc):
        accs = []
        for cc in range(d // LANES):
            sl = slice(cc * LANES, (cc + 1) * LANES)
            acc = jnp.broadcast_to(cb_ref[:, sl], (rc, LANES))
            for k in range(CONV_K):
                r0 = ci * rc + shift + k
                acc = acc + win_ref[r0:r0 + rc, sl] * w_ref[k:k + 1, sl]
            accs.append(acc)
        y = jnp.concatenate(accs, axis=1)
        mu = jnp.mean(y, axis=-1, keepdims=True)
        yc = y - mu
        var = jnp.mean(yc * yc, axis=-1, keepdims=True)
        yn = yc * lax.rsqrt(var + EPS) * lg + lb
        o_ref[0, ci * rc:(ci + 1) * rc, :] = (yn * _sigmoid(yn)).astype(bf16)


def _conv_call(z, dw, db, lg, lb, *, tm, n_lat, nq):
    b, t, d = z.shape
    rc = 32
    row = lambda bb, j: (0, 0)
    return pl.pallas_call(
        functools.partial(_conv_kernel, tm=tm, n_lat=n_lat, rc=rc),
        grid=(b, nq),
        in_specs=[
            pl.BlockSpec((1, t, d), lambda bb, j: (bb, 0, 0)),
            pl.BlockSpec((CONV_K, d), row),
            pl.BlockSpec((1, d), row),
            pl.BlockSpec((1, d), row),
            pl.BlockSpec((1, d), row),
        ],
        out_specs=pl.BlockSpec((1, tm, d), lambda bb, j: (bb, j, 0)),
        out_shape=jax.ShapeDtypeStruct((b, nq * tm, d), bf16),
        scratch_shapes=[pltpu.VMEM((tm + 2 * HALO, d), f32)],
        compiler_params=_params(("parallel", "arbitrary")),
        name="conv_ln_swish",
    )(z, dw, db, lg, lb)


def _merge_kernel(a_ref, cz_ref, gate_ref, x_ref, mod_ref, gf_ref, wao_ref, wco_ref, wout_ref,
                  xo_ref, h2_ref):
    d = x_ref.shape[-1]
    att = _dot(a_ref[0], wao_ref[...])
    cv = _dot(cz_ref[0], wco_ref[...])
    ga = gate_ref[0, :, 0:d].astype(f32)
    gc = gate_ref[0, :, d:2 * d].astype(f32)
    y = _dot((ga * att + gc * cv).astype(bf16), wout_ref[...])
    m = mod_ref[0]
    gt1, sh2, sc2 = m[:, 2 * d:3 * d], m[:, 3 * d:4 * d], m[:, 4 * d:5 * d]
    xn = x_ref[0] + gt1 * y
    xo_ref[0] = xn
    r = lax.rsqrt(jnp.mean(xn * xn, axis=-1, keepdims=True) + EPS)
    h2_ref[0] = ((xn * r * gf_ref[...]) * (1.0 + sc2) + sh2).astype(h2_ref.dtype)


def _merge_call(attn, cz, gates, x, mod, g_ffn, wao, wco, wout, *, tm, n_lat, nq, h2_dtype):
    b, t, d = x.shape
    nb = mod.shape[0] - 1
    tile = lambda bb, j: (bb, j, 0)
    return pl.pallas_call(
        _merge_kernel,
        grid=(b, nq),
        in_specs=[
            pl.BlockSpec((1, tm, attn.shape[-1]), tile),
            pl.BlockSpec((1, tm, d), tile),
            pl.BlockSpec((1, tm, 2 * d), tile),
            pl.BlockSpec((1, tm, d), tile),
            pl.BlockSpec((1, 1, N_MOD * d), lambda bb, j: (jnp.where(j < n_lat, bb, nb), 0, 0)),
            _resident((1, d)),
            _resident(wao.shape),
            _resident(wco.shape),
            _resident(wout.shape),
        ],
        out_specs=[pl.BlockSpec((1, tm, d), tile), pl.BlockSpec((1, tm, d), tile)],
        out_shape=[jax.ShapeDtypeStruct((b, t, d), f32), jax.ShapeDtypeStruct((b, nq * tm, d), h2_dtype)],
        input_output_aliases={3: 0},
        compiler_params=_params(("parallel", "parallel")),
        name="merge",
    )(attn, cz, gates, x, mod, g_ffn, wao, wco, wout)


def _ffn_kernel(h_ref, x_ref, mod_ref, wg_ref, wu_ref, wd_ref, xo_ref):
    d = x_ref.shape[-1]
    h = h_ref[0]
    g = _dot(h, wg_ref[...])
    u = _dot(h, wu_ref[...])
    y = _dot((g * _sigmoid(g) * u).astype(bf16), wd_ref[...])
    gt2 = mod_ref[0][:, 5 * d:6 * d]
    xo_ref[0] = x_ref[0] + gt2 * y


def _ffn_call(h2, x, mod, wg, wu, wd, *, tm, n_lat, nq):
    b, t, d = x.shape
    nb = mod.shape[0] - 1
    tile = lambda bb, j: (bb, j, 0)
    return pl.pallas_call(
        _ffn_kernel,
        grid=(b, nq),
        in_specs=[
            pl.BlockSpec((1, tm, d), tile),
            pl.BlockSpec((1, tm, d), tile),
            pl.BlockSpec((1, 1, N_MOD * d), lambda bb, j: (jnp.where(j < n_lat, bb, nb), 0, 0)),
            _resident(wg.shape),
            _resident(wu.shape),
            _resident(wd.shape),
        ],
        out_specs=pl.BlockSpec((1, tm, d), tile),
        out_shape=jax.ShapeDtypeStruct((b, t, d), f32),
        input_output_aliases={1: 0},
        compiler_params=_params(("parallel", "parallel")),
        name="dense_ffn",
    )(h2, x, mod, wg, wu, wd)


def _router_kernel(h_ref, whi_ref, wlo_ref, br_ref, tri_ref, eidx_ref, gate_ref, rank_ref, cnt_ref,
                   carry_ref):
    first = jnp.logical_and(pl.program_id(0) == 0, pl.program_id(1) == 0)

    @pl.when(first)
    def _():
        carry_ref[...] = jnp.zeros_like(carry_ref)

    h = h_ref[0]
    tm = h.shape[0]
    hi = h.astype(bf16)
    lo = (h - hi.astype(f32)).astype(bf16)
    whi = whi_ref[...]
    logits = _dot(hi, whi) + _dot(lo, whi) + _dot(hi, wlo_ref[...]) + br_ref[...]

    lane = lax.broadcasted_iota(jnp.int32, (tm, EXPERT_LANES), 1)
    m1 = jnp.max(logits, axis=-1, keepdims=True)
    i1 = jnp.min(jnp.where(logits == m1, lane, EXPERT_LANES), axis=-1, keepdims=True)
    rest = jnp.where(lane == i1, NEG_BIG, logits)
    m2 = jnp.max(rest, axis=-1, keepdims=True)
    i2 = jnp.min(jnp.where(rest == m2, lane, EXPERT_LANES), axis=-1, keepdims=True)
    tt = jnp.exp(m2 - m1)
    g1 = 1.0 / (1.0 + tt)
    g2 = tt / (1.0 + tt)

    sel1 = lane == i1
    sel2 = lane == i2
    memb = jnp.logical_or(sel1, sel2).astype(f32)
    rank_e = _dot(tri_ref[...], memb.astype(bf16)) + carry_ref[...]
    r1 = jnp.sum(jnp.where(sel1, rank_e, 0.0), axis=-1, keepdims=True)
    r2 = jnp.sum(jnp.where(sel2, rank_e, 0.0), axis=-1, keepdims=True)
    carry_ref[...] = carry_ref[...] + jnp.sum(memb, axis=0, keepdims=True)

    eidx_ref[0] = jnp.concatenate([i1, i2], axis=1)
    gate_ref[0] = jnp.concatenate([g1, g2], axis=1)
    rank_ref[0] = jnp.concatenate([r1, r2], axis=1).astype(jnp.int32)
    cnt_ref[...] = carry_ref[...]


def _router_call(h2, whi, wlo, br, tri, *, tm, nq):
    b, t, d = h2.shape
    tile = lambda bb, j: (bb, j, 0)
    return pl.pallas_call(
        _router_kernel,
        grid=(b, nq),
        in_specs=[
            pl.BlockSpec((1, tm, d), tile),
            _resident(whi.shape),
            _resident(wlo.shape),
            _resident(br.shape),
            _resident(tri.shape),
        ],
        out_specs=[
            pl.BlockSpec((1, tm, 2), tile),
            pl.BlockSpec((1, tm, 2), tile),
            pl.BlockSpec((1, tm, 2), tile),
            pl.BlockSpec((1, EXPERT_LANES), lambda bb, j: (0, 0)),
        ],
        out_shape=[
            jax.ShapeDtypeStruct((b, nq * tm, 2), jnp.int32),
            jax.ShapeDtypeStruct((b, nq * tm, 2), f32),
            jax.ShapeDtypeStruct((b, nq * tm, 2), jnp.int32),
            jax.ShapeDtypeStruct((1, EXPERT_LANES), f32),
        ],
        scratch_shapes=[pltpu.VMEM((1, EXPERT_LANES), f32)],
        compiler_params=_params(("arbitrary", "arbitrary")),
        name="router",
    )(h2, whi, wlo, br, tri)


def _dispatch_kernel(dest_ref, h_ref, xs_in_ref, xs_ref, sem):
    del xs_in_ref
    tm = h_ref.shape[1]

    def row_copy(i, dst_row):
        return pltpu.make_async_copy(h_ref.at[0, pl.ds(i, 1), :], xs_ref.at[pl.ds(dst_row, 1), :], sem)

    def issue(i, carry):
        row_copy(i, dest_ref[0, 0, i]).start()
        row_copy(i, dest_ref[0, 0, tm + i]).start()
        return carry

    def drain(i, carry):
        row_copy(0, 0).wait()
        row_copy(0, 0).wait()
        return carry

    lax.fori_loop(0, tm, issue, 0)
    lax.fori_loop(0, tm, drain, 0)


def _dispatch_call(dest, h2, xs_zero, *, tm, nq):
    b, t, d = h2.shape
    return pl.pallas_call(
        _dispatch_kernel,
        grid=(b, nq),
        in_specs=[
            pl.BlockSpec((1, 1, 2 * tm), lambda bb, j: (bb * nq + j, 0, 0), memory_space=pltpu.SMEM),
            pl.BlockSpec((1, tm, d), lambda bb, j: (bb, j, 0)),
            pl.BlockSpec(memory_space=pl.ANY),
        ],
        out_specs=pl.BlockSpec(memory_space=pl.ANY),
        out_shape=jax.ShapeDtypeStruct(xs_zero.shape, xs_zero.dtype),
        input_output_aliases={2: 0},
        scratch_shapes=[pltpu.SemaphoreType.DMA(())],
        compiler_params=_params(("arbitrary", "arbitrary")),
        name="moe_dispatch",
    )(dest, h2, xs_zero)


def _moe_ffn_kernel(be_ref, nu_ref, xs_ref, wg_ref, wu_ref, wd_ref, o_ref):
    del be_ref
    i = pl.program_id(0)
    f = pl.program_id(1)

    @pl.when(i < nu_ref[0])
    def _():
        xb = xs_ref[...].astype(bf16)
        g = _dot(xb, wg_ref[0])
        u = _dot(xb, wu_ref[0])
        y = _dot((g * _sigmoid(g) * u).astype(bf16), wd_ref[0])

        @pl.when(f == 0)
        def _():
            o_ref[...] = y

        @pl.when(f > 0)
        def _():
            o_ref[...] += y

    @pl.when(jnp.logical_and(i >= nu_ref[0], f == 0))
    def _():
        o_ref[...] = jnp.zeros_like(o_ref)


def _moe_ffn_call(block_e, n_used, xs, wg, wu, wd, *, blk, tf):
    rows, d = xs.shape
    n_e, _, dff = wg.shape
    nf = dff // tf

    def row_map(i, f, be, nu):
        return (jnp.minimum(i, nu[0] - 1), 0)

    def out_map(i, f, be, nu):
        return (i, 0)

    def up_map(i, f, be, nu):
        return (be[jnp.minimum(i, nu[0] - 1)], 0, jnp.where(i < nu[0], f, nf - 1))

    def down_map(i, f, be, nu):
        return (be[jnp.minimum(i, nu[0] - 1)], jnp.where(i < nu[0], f, nf - 1), 0)

    return pl.pallas_call(
        _moe_ffn_kernel,
        grid_spec=pltpu.PrefetchScalarGridSpec(
            num_scalar_prefetch=2,
            grid=(rows // blk, nf),
            in_specs=[
                pl.BlockSpec((blk, d), row_map),
                pl.BlockSpec((1, d, tf), up_map),
                pl.BlockSpec((1, d, tf), up_map),
                pl.BlockSpec((1, tf, d), down_map),
            ],
            out_specs=pl.BlockSpec((blk, d), out_map),
        ),
        out_shape=jax.ShapeDtypeStruct((rows, d), f32),
        compiler_params=_params(("arbitrary", "arbitrary")),
        name="moe_experts",
    )(block_e, n_used, xs, wg, wu, wd)


def _combine_kernel(dest_ref, ys_ref, x_ref, gate_ref, mod_ref, xo_ref, buf_ref, sem):
    tm, d = x_ref.shape[1], x_ref.shape[2]

    def row_copy(src_row, k, i):
        return pltpu.make_async_copy(ys_ref.at[pl.ds(src_row, 1), :], buf_ref.at[k, pl.ds(i, 1), :], sem)

    def issue(i, carry):
        row_copy(dest_ref[0, 0, i], 0, i).start()
        row_copy(dest_ref[0, 0, tm + i], 1, i).start()
        return carry

    def drain(i, carry):
        row_copy(0, 0, 0).wait()
        row_copy(0, 1, 0).wait()
        return carry

    lax.fori_loop(0, tm, issue, 0)
    lax.fori_loop(0, tm, drain, 0)

    g = gate_ref[0]
    y = g[:, 0:1] * buf_ref[0] + g[:, 1:2] * buf_ref[1]
    gt2 = mod_ref[0][:, 5 * d:6 * d]
    xo_ref[0] = x_ref[0] + gt2 * y


def _combine_call(dest, ys, x, gate, mod, *, tm, n_lat, nq):
    b, t, d = x.shape
    nb = mod.shape[0] - 1
    tile = lambda bb, j: (bb, j, 0)
    return pl.pallas_call(
        _combine_kernel,
        grid=(b, nq),
        in_specs=[
            pl.BlockSpec((1, 1, 2 * tm), lambda bb, j: (bb * nq + j, 0, 0), memory_space=pltpu.SMEM),
            pl.BlockSpec(memory_space=pl.ANY),
            pl.BlockSpec((1, tm, d), tile),
            pl.BlockSpec((1, tm, 2), tile),
            pl.BlockSpec((1, 1, N_MOD * d), lambda bb, j: (jnp.where(j < n_lat, bb, nb), 0, 0)),
        ],
        out_specs=pl.BlockSpec((1, tm, d), tile),
        out_shape=jax.ShapeDtypeStruct((b, t, d), f32),
        input_output_aliases={2: 0},
        scratch_shapes=[pltpu.VMEM((2, tm, d), f32), pltpu.SemaphoreType.DMA(())],
        compiler_params=_params(("arbitrary", "arbitrary")),
        name="moe_combine",
    )(dest, ys, x, gate, mod)


def _final_kernel(x_ref, g_ref, o_ref):
    x = x_ref[0]
    r = lax.rsqrt(jnp.mean(x * x, axis=-1, keepdims=True) + EPS)
    o_ref[0] = x * r * g_ref[...]


def _final_call(x, g, *, tm, s_len):
    b, t, d = x.shape
    tile = lambda bb, j: (bb, j, 0)
    return pl.pallas_call(
        _final_kernel,
        grid=(b, s_len // tm),
        in_specs=[pl.BlockSpec((1, tm, d), tile), _resident((1, d))],
        out_specs=pl.BlockSpec((1, tm, d), tile),
        out_shape=jax.ShapeDtypeStruct((b, s_len, d), f32),
        compiler_params=_params(("parallel", "parallel")),
        name="final_norm",
    )(x, g)


def _rope_tables(s_len, t_len):
    pairs = HEAD_DIM // 4
    pos = jnp.arange(s_len, dtype=jnp.int32)
    row = (pos // GRID_W).astype(f32)
    col = (pos % GRID_W).astype(f32)
    freqs = ROPE_THETA ** (-jnp.arange(pairs, dtype=f32) / pairs)
    ang = jnp.concatenate([row[:, None] * freqs, col[:, None] * freqs], axis=-1)
    cos, sin = jnp.cos(ang), jnp.sin(ang)
    cos_h = jnp.concatenate([cos, cos], axis=-1)
    sin_h = jnp.concatenate([-sin, sin], axis=-1)
    reps = LANES // HEAD_DIM
    cos_t = jnp.tile(cos_h, (1, reps))
    sin_t = jnp.tile(sin_h, (1, reps))
    n_ctx = t_len - s_len
    cos_t = jnp.concatenate([cos_t, jnp.ones((n_ctx, LANES), f32)], axis=0)
    sin_t = jnp.concatenate([sin_t, jnp.zeros((n_ctx, LANES), f32)], axis=0)
    return cos_t, sin_t


def _moe_layer(x, h2, mod, w_router, b_router, wg, wu, wd, tri, *, tm, n_lat, nq):
    b, t, d = x.shape
    n_e = w_router.shape[1]
    dff = wg.shape[-1]
    n_tok = b * nq * tm
    blk = 512 if n_tok * 2 >= 8192 else 128
    tf = 512 if dff % 512 == 0 else 256

    wr = jnp.zeros((d, EXPERT_LANES), f32).at[:, :n_e].set(w_router)
    whi = wr.astype(bf16)
    wlo = (wr - whi.astype(f32)).astype(bf16)
    br = jnp.full((1, EXPERT_LANES), NEG_BIG, f32).at[0, :n_e].set(b_router)

    eidx, gate, rank, cnt = _router_call(h2, whi, wlo, br, tri, tm=tm, nq=nq)

    counts = cnt[0, :n_e].astype(jnp.int32)
    pcounts = (counts + blk - 1) // blk * blk
    pends = jnp.cumsum(pcounts)
    pstarts = pends - pcounts
    n_blocks = -(-(n_tok * 2) // blk) + n_e
    rows = n_blocks * blk
    used = nq * tm
    e_used = eidx[:, :used]
    dest = rank[:, :used]
    for e in range(n_e):
        dest = dest + jnp.where(e_used == e, pstarts[e], 0)
    dest = dest.reshape(b, nq, tm, 2).transpose(0, 1, 3, 2).reshape(b * nq, 1, 2 * tm)
    block_start = jnp.arange(n_blocks, dtype=jnp.int32) * blk
    block_e = jnp.minimum(jnp.sum(block_start[:, None] >= pends[None, :], axis=1), n_e - 1).astype(jnp.int32)
    n_used = (pends[-1] // blk).astype(jnp.int32).reshape(1)

    xs = _dispatch_call(dest, h2, jnp.zeros((rows, d), f32), tm=tm, nq=nq)
    ys = _moe_ffn_call(block_e, n_used, xs, wg, wu, wd, blk=blk, tf=tf)
    return _combine_call(dest, ys, x, gate, mod, tm=tm, n_lat=n_lat, nq=nq)


def kernel(x, c, ctx, c_ctx, w_ada, b_ada, g_mix, g_ffn, w_in, q_norm, k_norm, w_attn_o, conv_dw, conv_b, conv_ln_g, conv_ln_b, w_conv_o, w_out, w_ff_gate, w_ff_up, w_ff_down, w_router, b_router, w_moe_gate, w_moe_up, w_moe_down, g_final):
    b, s_len, d = x.shape
    n_ctx = ctx.shape[1]
    t = s_len + n_ctx
    depth = w_ada.shape[0]
    tm = 256 if (s_len % 256 == 0 and n_ctx % 256 == 0) else 128
    n_lat = s_len // tm
    nt = t // tm

    cvec = jnp.concatenate([c, c_ctx[None, :]], axis=0)
    mod_all = _ada_call(cvec, w_ada, b_ada)

    cos_t, sin_t = _rope_tables(s_len, t)
    head = lax.broadcasted_iota(jnp.int32, (LANES, LANES), 0) // HEAD_DIM
    seg = (head == head.T).astype(bf16)
    ti = lax.broadcasted_iota(jnp.int32, (tm, tm), 0)
    tri = (ti.T < ti).astype(bf16)
    reps = LANES // HEAD_DIM

    xs = jnp.concatenate([x, ctx], axis=1)
    for l in range(depth):
        last = l == depth - 1
        nq = n_lat if last else nt
        mod = mod_all[l][:, None, :]
        q, kt, v, z, gates = _inproj_call(
            xs, mod, g_mix[l][None, :], w_in[l].astype(bf16),
            jnp.tile(q_norm[l], reps)[None, :], jnp.tile(k_norm[l], reps)[None, :],
            cos_t, sin_t, seg, tm=tm, n_lat=n_lat)
        attn = _attn_call(q, kt, v, tm=tm, n_lat=n_lat, nq=nq, s_len=s_len)
        cz = _conv_call(z, conv_dw[l], conv_b[l][None, :], conv_ln_g[l][None, :], conv_ln_b[l][None, :],
                        tm=tm, n_lat=n_lat, nq=nq)
        moe = l % 2 == 1
        xs, h2 = _merge_call(attn, cz, gates, xs, mod, g_ffn[l][None, :], w_attn_o[l].astype(bf16),
                             w_conv_o[l].astype(bf16), w_out[l].astype(bf16),
                             tm=tm, n_lat=n_lat, nq=nq, h2_dtype=f32 if moe else bf16)
        i = l // 2
        if moe:
            xs = _moe_layer(xs, h2, mod, w_router[i], b_router[i], w_moe_gate[i].astype(bf16),
                            w_moe_up[i].astype(bf16), w_moe_down[i].astype(bf16), tri,
                            tm=tm, n_lat=n_lat, nq=nq)
        else:
            xs = _ffn_call(h2, xs, mod, w_ff_gate[i].astype(bf16), w_ff_up[i].astype(bf16),
                           w_ff_down[i].astype(bf16), tm=tm, n_lat=n_lat, nq=nq)
    return _final_call(xs, g_final[None, :], tm=tm, s_len=s_len)
```

```python
import functools

import jax
import jax.numpy as jnp
from jax import lax
from jax.experimental import pallas as pl
from jax.experimental.pallas import tpu as pltpu

N_HEADS = 16
N_KV_HEADS = 4
HEAD_DIM = 64
GROUP = N_HEADS // N_KV_HEADS
GRID_W = 64
ROPE_THETA = 10000.0
CONV_K = 31
CONV_PAD = (CONV_K - 1) // 2
N_MOD = 6
EPS = 1e-6
LANES = 128
SUBLANES = 8
LOG2E = 1.4426950408889634
HALO = 16
EXPERT_LANES = 128
NEG_BIG = -3.0e38
VMEM_LIMIT = 56 * 1024 * 1024

f32 = jnp.float32
bf16 = jnp.bfloat16


def _sigmoid(x):
    return 1.0 / (1.0 + jnp.exp(-x))


def _dot(a, b):
    return jnp.dot(a, b, preferred_element_type=f32)


def _resident(shape):
    zeros = (0,) * len(shape)
    return pl.BlockSpec(shape, lambda *_: zeros, pipeline_mode=pl.Buffered(1))


def _params(sem, vmem=VMEM_LIMIT):
    return pltpu.CompilerParams(dimension_semantics=sem, vmem_limit_bytes=vmem)


def _ada_kernel(c_ref, w_ref, b_ref, o_ref):
    cv = c_ref[...]
    a = (cv * _sigmoid(cv)).astype(bf16)
    o_ref[0] = _dot(a, w_ref[0].astype(bf16)) + b_ref[0]


def _ada_call(cvec, w_ada, b_ada):
    depth, d, n = w_ada.shape
    r = cvec.shape[0]
    tn = 1536
    return pl.pallas_call(
        _ada_kernel,
        grid=(depth, n // tn),
        in_specs=[
            pl.BlockSpec((r, d), lambda l, t: (0, 0)),
            pl.BlockSpec((1, d, tn), lambda l, t: (l, 0, t)),
            pl.BlockSpec((1, 1, tn), lambda l, t: (l, 0, t)),
        ],
        out_specs=pl.BlockSpec((1, r, tn), lambda l, t: (l, 0, t)),
        out_shape=jax.ShapeDtypeStruct((depth, r, n), f32),
        compiler_params=_params(("arbitrary", "arbitrary")),
        name="ada_mod",
    )(cvec, w_ada, b_ada.reshape(depth, 1, n))


def _inproj_kernel(x_ref, mod_ref, g_ref, w_ref, qn_ref, kn_ref, cos_ref, sin_ref, seg_ref,
                   q_ref, kt_ref, v_ref, z_ref, gate_ref):
    d = x_ref.shape[-1]
    q_w = N_HEADS * HEAD_DIM
    kv_w = N_KV_HEADS * HEAD_DIM
    off_k, off_v, off_conv = q_w, q_w + kv_w, q_w + 2 * kv_w
    off_gate = off_conv + 2 * d

    x = x_ref[0]
    m = mod_ref[0]
    sh, sc = m[:, 0:d], m[:, d:2 * d]
    r = lax.rsqrt(jnp.mean(x * x, axis=-1, keepdims=True) + EPS)
    h = ((x * r * g_ref[...]) * (1.0 + sc) + sh).astype(bf16)

    def proj(lo, hi):
        return _dot(h, w_ref[:, lo:hi])

    seg = seg_ref[...]
    lane = lax.broadcasted_iota(jnp.int32, (1, LANES), 1)
    first_half = (lane % HEAD_DIM) < (HEAD_DIM // 2)
    cos = cos_ref[...]
    sin = sin_ref[...]

    def norm_rope(chunk, gain):
        c2 = chunk * chunk
        hi = c2.astype(bf16)
        lo = (c2 - hi.astype(f32)).astype(bf16)
        ss = _dot(hi, seg) + _dot(lo, seg)
        y = chunk * lax.rsqrt(ss * (1.0 / HEAD_DIM) + EPS) * gain
        rot = jnp.where(first_half, pltpu.roll(y, LANES - HEAD_DIM // 2, 1),
                        pltpu.roll(y, HEAD_DIM // 2, 1))
        return y * cos + rot * sin

    pq = proj(0, off_k)
    qn = qn_ref[...]
    scale = HEAD_DIM ** -0.5 * LOG2E
    for c in range(q_w // LANES):
        sl = slice(c * LANES, (c + 1) * LANES)
        q_ref[0, :, sl] = (norm_rope(pq[:, sl], qn) * scale).astype(bf16)

    pk = proj(off_k, off_v)
    kn = kn_ref[...]
    kk = jnp.concatenate([norm_rope(pk[:, c * LANES:(c + 1) * LANES], kn)
                          for c in range(kv_w // LANES)], axis=1)
    kt_ref[0] = kk.T.astype(bf16)

    pv = proj(off_v, off_conv)
    for hh in range(N_KV_HEADS):
        v_ref[0, hh] = pv[:, hh * HEAD_DIM:(hh + 1) * HEAD_DIM].astype(bf16)

    a = proj(off_conv, off_conv + d)
    g = proj(off_conv + d, off_gate)
    z_ref[0] = (a * _sigmoid(g)).astype(bf16)
    gate_ref[0] = _sigmoid(proj(off_gate, off_gate + 2 * d)).astype(bf16)


def _inproj_call(x, mod, g_mix, w_in, qn, kn, cos_t, sin_t, seg, *, tm, n_lat):
    b, t, d = x.shape
    nt = t // tm
    in_w = w_in.shape[1]
    q_w, kv_w = N_HEADS * HEAD_DIM, N_KV_HEADS * HEAD_DIM
    nb = mod.shape[0] - 1
    tile = lambda bb, j: (bb, j, 0)
    return pl.pallas_call(
        _inproj_kernel,
        grid=(b, nt),
        in_specs=[
            pl.BlockSpec((1, tm, d), tile),
            pl.BlockSpec((1, 1, N_MOD * d), lambda bb, j: (jnp.where(j < n_lat, bb, nb), 0, 0)),
            _resident((1, d)),
            _resident((d, in_w)),
            _resident((1, LANES)),
            _resident((1, LANES)),
            pl.BlockSpec((tm, LANES), lambda bb, j: (j, 0)),
            pl.BlockSpec((tm, LANES), lambda bb, j: (j, 0)),
            _resident((LANES, LANES)),
        ],
        out_specs=[
            pl.BlockSpec((1, tm, q_w), tile),
            pl.BlockSpec((1, kv_w, tm), lambda bb, j: (bb, 0, j)),
            pl.BlockSpec((1, N_KV_HEADS, tm, HEAD_DIM), lambda bb, j: (bb, 0, j, 0)),
            pl.BlockSpec((1, tm, d), tile),
            pl.BlockSpec((1, tm, 2 * d), tile),
        ],
        out_shape=[
            jax.ShapeDtypeStruct((b, t, q_w), bf16),
            jax.ShapeDtypeStruct((b, kv_w, t), bf16),
            jax.ShapeDtypeStruct((b, N_KV_HEADS, t, HEAD_DIM), bf16),
            jax.ShapeDtypeStruct((b, t, d), bf16),
            jax.ShapeDtypeStruct((b, t, 2 * d), bf16),
        ],
        compiler_params=_params(("parallel", "parallel")),
        name="in_proj",
    )(x, mod, g_mix, w_in, qn, kn, cos_t, sin_t, seg)


def _attn_kernel(q_ref, kt_ref, v_ref, o_ref, *, n_lat, s_len):
    j = pl.program_id(2)
    t = kt_ref.shape[-1]

    def run(k0):
        kt = kt_ref[0, :, k0:t]
        v = v_ref[0, 0, k0:t, :]
        def qk(g):
            return _dot(q_ref[0, :, g * HEAD_DIM:(g + 1) * HEAD_DIM], kt)

        scores = [qk(0), qk(1)]
        outs = []
        for g in range(GROUP):
            s = scores[g]
            mx = jnp.max(s, axis=-1, keepdims=True)
            p = jnp.exp2(s - mx)
            den = jnp.sum(p, axis=-1, keepdims=True)
            if g + 2 < GROUP:
                scores.append(qk(g + 2))
            outs.append(_dot(p.astype(bf16), v) / den)
        o_ref[0] = jnp.concatenate(outs, axis=1).astype(bf16)

    @pl.when(j < n_lat)
    def _():
        run(0)

    @pl.when(j >= n_lat)
    def _():
        run(s_len)


def _attn_call(q, kt, v, *, tm, n_lat, nq, s_len):
    b, t, q_w = q.shape
    gw = GROUP * HEAD_DIM
    return pl.pallas_call(
        functools.partial(_attn_kernel, n_lat=n_lat, s_len=s_len),
        grid=(b, N_KV_HEADS, nq),
        in_specs=[
            pl.BlockSpec((1, tm, gw), lambda bb, h, j: (bb, j, h)),
            pl.BlockSpec((1, HEAD_DIM, t), lambda bb, h, j: (bb, h, 0)),
            pl.BlockSpec((1, 1, t, HEAD_DIM), lambda bb, h, j: (bb, h, 0, 0)),
        ],
        out_specs=pl.BlockSpec((1, tm, gw), lambda bb, h, j: (bb, j, h)),
        out_shape=jax.ShapeDtypeStruct((b, nq * tm, q_w), bf16),
        compiler_params=_params(("parallel", "parallel", "parallel")),
        name="attention",
    )(q, kt, v)


def _conv_kernel(z_ref, w_ref, cb_ref, lg_ref, lb_ref, o_ref, win_ref, *, tm, n_lat, rc):
    j = pl.program_id(1)
    t, d = z_ref.shape[1], z_ref.shape[2]
    nt = t // tm

    row0 = pl.multiple_of(j * tm, tm)
    win_ref[HALO:HALO + tm] = z_ref[0, pl.ds(row0, tm), :].astype(f32)
    seg_start = jnp.logical_or(j == 0, j == n_lat)
    seg_end = jnp.logical_or(j == n_lat - 1, j == nt - 1)

    @pl.when(seg_start)
    def _():
        win_ref[0:HALO] = jnp.zeros((HALO, d), f32)

    @pl.when(jnp.logical_not(seg_start))
    def _():
        win_ref[0:HALO] = z_ref[0, pl.ds(pl.multiple_of(row0 - HALO, HALO), HALO), :].astype(f32)

    @pl.when(seg_end)
    def _():
        win_ref[HALO + tm:2 * HALO + tm] = jnp.zeros((HALO, d), f32)

    @pl.when(jnp.logical_not(seg_end))
    def _():
        win_ref[HALO + tm:2 * HALO + tm] = z_ref[0, pl.ds(pl.multiple_of(row0 + tm, HALO), HALO), :].astype(f32)

    lg = lg_ref[...]
    lb = lb_ref[...]
    shift = HALO - CONV_PAD
    for ci in range(tm // rc):
        accs = []
        for cc in range(d // LANES):
            sl = slice(cc * LANES, (cc + 1) * LANES)
            acc = jnp.broadcast_to(cb_ref[:, sl], (rc, LANES))
            for r in range(SUBLANES):
                ext = rc + (SUBLANES if r else 0)
                part = None
                for k in range(CONV_K):
                    if (k + shift) % SUBLANES != r:
                        continue
                    r0 = ci * rc + k + shift - r
                    term = win_ref[r0:r0 + ext, sl] * w_ref[k:k + 1, sl]
                    part = term if part is None else part + term
                acc = acc + part[r:r + rc]
            accs.append(acc)
        y = jnp.concatenate(accs, axis=1)
        mu = jnp.mean(y, axis=-1, keepdims=True)
        yc = y - mu
        var = jnp.mean(yc * yc, axis=-1, keepdims=True)
        yn = yc * lax.rsqrt(var + EPS) * lg + lb
        o_ref[0, ci * rc:(ci + 1) * rc, :] = (yn * _sigmoid(yn)).astype(bf16)


def _conv_call(z, dw, db, lg, lb, *, tm, n_lat, nq):
    b, t, d = z.shape
    rc = 64
    row = lambda bb, j: (0, 0)
    return pl.pallas_call(
        functools.partial(_conv_kernel, tm=tm, n_lat=n_lat, rc=rc),
        grid=(b, nq),
        in_specs=[
            pl.BlockSpec((1, t, d), lambda bb, j: (bb, 0, 0)),
            pl.BlockSpec((CONV_K, d), row),
            pl.BlockSpec((1, d), row),
            pl.BlockSpec((1, d), row),
            pl.BlockSpec((1, d), row),
        ],
        out_specs=pl.BlockSpec((1, tm, d), lambda bb, j: (bb, j, 0)),
        out_shape=jax.ShapeDtypeStruct((b, nq * tm, d), bf16),
        scratch_shapes=[pltpu.VMEM((tm + 2 * HALO, d), f32)],
        compiler_params=_params(("parallel", "arbitrary")),
        name="conv_ln_swish",
    )(z, dw, db, lg, lb)


def _merge_kernel(a_ref, cz_ref, gate_ref, x_ref, mod_ref, gf_ref, wao_ref, wco_ref, wout_ref,
                  xo_ref, h2_ref):
    d = x_ref.shape[-1]
    att = _dot(a_ref[0], wao_ref[...])
    cv = _dot(cz_ref[0], wco_ref[...])
    ga = gate_ref[0, :, 0:d].astype(f32)
    gc = gate_ref[0, :, d:2 * d].astype(f32)
    y = _dot((ga * att + gc * cv).astype(bf16), wout_ref[...])
    m = mod_ref[0]
    gt1, sh2, sc2 = m[:, 2 * d:3 * d], m[:, 3 * d:4 * d], m[:, 4 * d:5 * d]
    xn = x_ref[0] + gt1 * y
    xo_ref[0] = xn
    r = lax.rsqrt(jnp.mean(xn * xn, axis=-1, keepdims=True) + EPS)
    h2_ref[0] = ((xn * r * gf_ref[...]) * (1.0 + sc2) + sh2).astype(h2_ref.dtype)


def _merge_call(attn, cz, gates, x, mod, g_ffn, wao, wco, wout, *, tm, n_lat, nq, h2_dtype):
    b, t, d = x.shape
    nb = mod.shape[0] - 1
    tile = lambda bb, j: (bb, j, 0)
    return pl.pallas_call(
        _merge_kernel,
        grid=(b, nq),
        in_specs=[
            pl.BlockSpec((1, tm, attn.shape[-1]), tile),
            pl.BlockSpec((1, tm, d), tile),
            pl.BlockSpec((1, tm, 2 * d), tile),
            pl.BlockSpec((1, tm, d), tile),
            pl.BlockSpec((1, 1, N_MOD * d), lambda bb, j: (jnp.where(j < n_lat, bb, nb), 0, 0)),
            _resident((1, d)),
            _resident(wao.shape),
            _resident(wco.shape),
            _resident(wout.shape),
        ],
        out_specs=[pl.BlockSpec((1, tm, d), tile), pl.BlockSpec((1, tm, d), tile)],
        out_shape=[jax.ShapeDtypeStruct((b, t, d), f32), jax.ShapeDtypeStruct((b, nq * tm, d), h2_dtype)],
        input_output_aliases={3: 0},
        compiler_params=_params(("parallel", "parallel")),
        name="merge",
    )(attn, cz, gates, x, mod, g_ffn, wao, wco, wout)


def _ffn_kernel(h_ref, x_ref, mod_ref, wg_ref, wu_ref, wd_ref, xo_ref):
    d = x_ref.shape[-1]
    h = h_ref[0]
    g = _dot(h, wg_ref[...])
    u = _dot(h, wu_ref[...])
    y = _dot((g * _sigmoid(g) * u).astype(bf16), wd_ref[...])
    gt2 = mod_ref[0][:, 5 * d:6 * d]
    xo_ref[0] = x_ref[0] + gt2 * y


def _ffn_call(h2, x, mod, wg, wu, wd, *, tm, n_lat, nq):
    b, t, d = x.shape
    nb = mod.shape[0] - 1
    tile = lambda bb, j: (bb, j, 0)
    return pl.pallas_call(
        _ffn_kernel,
        grid=(b, nq),
        in_specs=[
            pl.BlockSpec((1, tm, d), tile),
            pl.BlockSpec((1, tm, d), tile),
            pl.BlockSpec((1, 1, N_MOD * d), lambda bb, j: (jnp.where(j < n_lat, bb, nb), 0, 0)),
            _resident(wg.shape),
            _resident(wu.shape),
            _resident(wd.shape),
        ],
        out_specs=pl.BlockSpec((1, tm, d), tile),
        out_shape=jax.ShapeDtypeStruct((b, t, d), f32),
        input_output_aliases={1: 0},
        compiler_params=_params(("parallel", "parallel")),
        name="dense_ffn",
    )(h2, x, mod, wg, wu, wd)


def _router_kernel(h_ref, whi_ref, wlo_ref, br_ref, tri_ref, eidx_ref, gate_ref, rank_ref, cnt_ref,
                   carry_ref):
    first = jnp.logical_and(pl.program_id(0) == 0, pl.program_id(1) == 0)

    @pl.when(first)
    def _():
        carry_ref[...] = jnp.zeros_like(carry_ref)

    h = h_ref[0]
    tm = h.shape[0]
    hi = h.astype(bf16)
    lo = (h - hi.astype(f32)).astype(bf16)
    whi = whi_ref[...]
    logits = _dot(hi, whi) + _dot(lo, whi) + _dot(hi, wlo_ref[...]) + br_ref[...]

    lane = lax.broadcasted_iota(jnp.int32, (tm, EXPERT_LANES), 1)
    m1 = jnp.max(logits, axis=-1, keepdims=True)
    i1 = jnp.min(jnp.where(logits == m1, lane, EXPERT_LANES), axis=-1, keepdims=True)
    rest = jnp.where(lane == i1, NEG_BIG, logits)
    m2 = jnp.max(rest, axis=-1, keepdims=True)
    i2 = jnp.min(jnp.where(rest == m2, lane, EXPERT_LANES), axis=-1, keepdims=True)
    tt = jnp.exp(m2 - m1)
    g1 = 1.0 / (1.0 + tt)
    g2 = tt / (1.0 + tt)

    sel1 = lane == i1
    sel2 = lane == i2
    memb = jnp.logical_or(sel1, sel2).astype(f32)
    rank_e = _dot(tri_ref[...], memb.astype(bf16)) + carry_ref[...]
    r1 = jnp.sum(jnp.where(sel1, rank_e, 0.0), axis=-1, keepdims=True)
    r2 = jnp.sum(jnp.where(sel2, rank_e, 0.0), axis=-1, keepdims=True)
    carry_ref[...] = carry_ref[...] + jnp.sum(memb, axis=0, keepdims=True)

    eidx_ref[0] = jnp.concatenate([i1, i2], axis=1)
    gate_ref[0] = jnp.concatenate([g1, g2], axis=1)
    rank_ref[0] = jnp.concatenate([r1, r2], axis=1).astype(jnp.int32)
    cnt_ref[...] = carry_ref[...]


def _router_call(h2, whi, wlo, br, tri, *, tm, nq):
    b, t, d = h2.shape
    tile = lambda bb, j: (bb, j, 0)
    return pl.pallas_call(
        _router_kernel,
        grid=(b, nq),
        in_specs=[
            pl.BlockSpec((1, tm, d), tile),
            _resident(whi.shape),
            _resident(wlo.shape),
            _resident(br.shape),
            _resident(tri.shape),
        ],
        out_specs=[
            pl.BlockSpec((1, tm, 2), tile),
            pl.BlockSpec((1, tm, 2), tile),
            pl.BlockSpec((1, tm, 2), tile),
            pl.BlockSpec((1, EXPERT_LANES), lambda bb, j: (0, 0)),
        ],
        out_shape=[
            jax.ShapeDtypeStruct((b, nq * tm, 2), jnp.int32),
            jax.ShapeDtypeStruct((b, nq * tm, 2), f32),
            jax.ShapeDtypeStruct((b, nq * tm, 2), jnp.int32),
            jax.ShapeDtypeStruct((1, EXPERT_LANES), f32),
        ],
        scratch_shapes=[pltpu.VMEM((1, EXPERT_LANES), f32)],
        compiler_params=_params(("arbitrary", "arbitrary")),
        name="router",
    )(h2, whi, wlo, br, tri)


def _dispatch_kernel(dest_ref, h_ref, xs_in_ref, xs_ref, sem):
    del xs_in_ref
    tm = h_ref.shape[1]

    def row_copy(i, dst_row):
        return pltpu.make_async_copy(h_ref.at[0, pl.ds(i, 1), :], xs_ref.at[pl.ds(dst_row, 1), :], sem)

    def issue(i, carry):
        row_copy(i, dest_ref[0, 0, i]).start()
        row_copy(i, dest_ref[0, 0, tm + i]).start()
        return carry

    def drain(i, carry):
        row_copy(0, 0).wait()
        row_copy(0, 0).wait()
        return carry

    lax.fori_loop(0, tm, issue, 0)
    lax.fori_loop(0, tm, drain, 0)


def _dispatch_call(dest, h2, xs_zero, *, tm, nq):
    b, t, d = h2.shape
    return pl.pallas_call(
        _dispatch_kernel,
        grid=(b, nq),
        in_specs=[
            pl.BlockSpec((1, 1, 2 * tm), lambda bb, j: (bb * nq + j, 0, 0), memory_space=pltpu.SMEM),
            pl.BlockSpec((1, tm, d), lambda bb, j: (bb, j, 0)),
            pl.BlockSpec(memory_space=pl.ANY),
        ],
        out_specs=pl.BlockSpec(memory_space=pl.ANY),
        out_shape=jax.ShapeDtypeStruct(xs_zero.shape, xs_zero.dtype),
        input_output_aliases={2: 0},
        scratch_shapes=[pltpu.SemaphoreType.DMA(())],
        compiler_params=_params(("arbitrary", "arbitrary")),
        name="moe_dispatch",
    )(dest, h2, xs_zero)


def _moe_ffn_kernel(be_ref, nu_ref, xs_ref, wg_ref, wu_ref, wd_ref, o_ref):
    del be_ref
    i = pl.program_id(0)
    f = pl.program_id(1)

    @pl.when(i < nu_ref[0])
    def _():
        xb = xs_ref[...].astype(bf16)
        g = _dot(xb, wg_ref[0])
        u = _dot(xb, wu_ref[0])
        y = _dot((g * _sigmoid(g) * u).astype(bf16), wd_ref[0])

        @pl.when(f == 0)
        def _():
            o_ref[...] = y

        @pl.when(f > 0)
        def _():
            o_ref[...] += y

    @pl.when(jnp.logical_and(i >= nu_ref[0], f == 0))
    def _():
        o_ref[...] = jnp.zeros_like(o_ref)


def _moe_ffn_call(block_e, n_used, xs, wg, wu, wd, *, blk, tf):
    rows, d = xs.shape
    n_e, _, dff = wg.shape
    nf = dff // tf

    def row_map(i, f, be, nu):
        return (jnp.minimum(i, nu[0] - 1), 0)

    def out_map(i, f, be, nu):
        return (i, 0)

    def up_map(i, f, be, nu):
        return (be[jnp.minimum(i, nu[0] - 1)], 0, jnp.where(i < nu[0], f, nf - 1))

    def down_map(i, f, be, nu):
        return (be[jnp.minimum(i, nu[0] - 1)], jnp.where(i < nu[0], f, nf - 1), 0)

    return pl.pallas_call(
        _moe_ffn_kernel,
        grid_spec=pltpu.PrefetchScalarGridSpec(
            num_scalar_prefetch=2,
            grid=(rows // blk, nf),
            in_specs=[
                pl.BlockSpec((blk, d), row_map),
                pl.BlockSpec((1, d, tf), up_map),
                pl.BlockSpec((1, d, tf), up_map),
                pl.BlockSpec((1, tf, d), down_map),
            ],
            out_specs=pl.BlockSpec((blk, d), out_map),
        ),
        out_shape=jax.ShapeDtypeStruct((rows, d), f32),
        compiler_params=_params(("arbitrary", "arbitrary")),
        name="moe_experts",
    )(block_e, n_used, xs, wg, wu, wd)


def _combine_kernel(dest_ref, ys_ref, x_ref, gate_ref, mod_ref, xo_ref, buf_ref, sem):
    tm, d = x_ref.shape[1], x_ref.shape[2]

    def row_copy(src_row, k, i):
        return pltpu.make_async_copy(ys_ref.at[pl.ds(src_row, 1), :], buf_ref.at[k, pl.ds(i, 1), :], sem)

    def issue(i, carry):
        row_copy(dest_ref[0, 0, i], 0, i).start()
        row_copy(dest_ref[0, 0, tm + i], 1, i).start()
        return carry

    def drain(i, carry):
        row_copy(0, 0, 0).wait()
        row_copy(0, 1, 0).wait()
        return carry

    lax.fori_loop(0, tm, issue, 0)
    lax.fori_loop(0, tm, drain, 0)

    g = gate_ref[0]
    y = g[:, 0:1] * buf_ref[0] + g[:, 1:2] * buf_ref[1]
    gt2 = mod_ref[0][:, 5 * d:6 * d]
    xo_ref[0] = x_ref[0] + gt2 * y


def _combine_call(dest, ys, x, gate, mod, *, tm, n_lat, nq):
    b, t, d = x.shape
    nb = mod.shape[0] - 1
    tile = lambda bb, j: (bb, j, 0)
    return pl.pallas_call(
        _combine_kernel,
        grid=(b, nq),
        in_specs=[
            pl.BlockSpec((1, 1, 2 * tm), lambda bb, j: (bb * nq + j, 0, 0), memory_space=pltpu.SMEM),
            pl.BlockSpec(memory_space=pl.ANY),
            pl.BlockSpec((1, tm, d), tile),
            pl.BlockSpec((1, tm, 2), tile),
            pl.BlockSpec((1, 1, N_MOD * d), lambda bb, j: (jnp.where(j < n_lat, bb, nb), 0, 0)),
        ],
        out_specs=pl.BlockSpec((1, tm, d), tile),
        out_shape=jax.ShapeDtypeStruct((b, t, d), f32),
        input_output_aliases={2: 0},
        scratch_shapes=[pltpu.VMEM((2, tm, d), f32), pltpu.SemaphoreType.DMA(())],
        compiler_params=_params(("arbitrary", "arbitrary")),
        name="moe_combine",
    )(dest, ys, x, gate, mod)


def _final_kernel(x_ref, g_ref, o_ref):
    x = x_ref[0]
    r = lax.rsqrt(jnp.mean(x * x, axis=-1, keepdims=True) + EPS)
    o_ref[0] = x * r * g_ref[...]


def _final_call(x, g, *, tm, s_len):
    b, t, d = x.shape
    tile = lambda bb, j: (bb, j, 0)
    return pl.pallas_call(
        _final_kernel,
        grid=(b, s_len // tm),
        in_specs=[pl.BlockSpec((1, tm, d), tile), _resident((1, d))],
        out_specs=pl.BlockSpec((1, tm, d), tile),
        out_shape=jax.ShapeDtypeStruct((b, s_len, d), f32),
        compiler_params=_params(("parallel", "parallel")),
        name="final_norm",
    )(x, g)


def _rope_tables(s_len, t_len):
    pairs = HEAD_DIM // 4
    pos = jnp.arange(s_len, dtype=jnp.int32)
    row = (pos // GRID_W).astype(f32)
    col = (pos % GRID_W).astype(f32)
    freqs = ROPE_THETA ** (-jnp.arange(pairs, dtype=f32) / pairs)
    ang = jnp.concatenate([row[:, None] * freqs, col[:, None] * freqs], axis=-1)
    cos, sin = jnp.cos(ang), jnp.sin(ang)
    cos_h = jnp.concatenate([cos, cos], axis=-1)
    sin_h = jnp.concatenate([-sin, sin], axis=-1)
    reps = LANES // HEAD_DIM
    cos_t = jnp.tile(cos_h, (1, reps))
    sin_t = jnp.tile(sin_h, (1, reps))
    n_ctx = t_len - s_len
    cos_t = jnp.concatenate([cos_t, jnp.ones((n_ctx, LANES), f32)], axis=0)
    sin_t = jnp.concatenate([sin_t, jnp.zeros((n_ctx, LANES), f32)], axis=0)
    return cos_t, sin_t


def _moe_layer(x, h2, mod, w_router, b_router, wg, wu, wd, tri, *, tm, n_lat, nq):
    b, t, d = x.shape
    n_e = w_router.shape[1]
    dff = wg.shape[-1]
    n_tok = b * nq * tm
    blk = 512 if n_tok * 2 >= 8192 else 128
    tf = 512 if dff % 512 == 0 else 256

    wr = jnp.zeros((d, EXPERT_LANES), f32).at[:, :n_e].set(w_router)
    whi = wr.astype(bf16)
    wlo = (wr - whi.astype(f32)).astype(bf16)
    br = jnp.full((1, EXPERT_LANES), NEG_BIG, f32).at[0, :n_e].set(b_router)

    eidx, gate, rank, cnt = _router_call(h2, whi, wlo, br, tri, tm=tm, nq=nq)

    counts = cnt[0, :n_e].astype(jnp.int32)
    pcounts = (counts + blk - 1) // blk * blk
    pends = jnp.cumsum(pcounts)
    pstarts = pends - pcounts
    n_blocks = -(-(n_tok * 2) // blk) + n_e
    rows = n_blocks * blk
    used = nq * tm
    e_used = eidx[:, :used]
    dest = rank[:, :used]
    for e in range(n_e):
        dest = dest + jnp.where(e_used == e, pstarts[e], 0)
    dest = dest.reshape(b, nq, tm, 2).transpose(0, 1, 3, 2).reshape(b * nq, 1, 2 * tm)
    block_start = jnp.arange(n_blocks, dtype=jnp.int32) * blk
    block_e = jnp.minimum(jnp.sum(block_start[:, None] >= pends[None, :], axis=1), n_e - 1).astype(jnp.int32)
    n_used = (pends[-1] // blk).astype(jnp.int32).reshape(1)

    xs = _dispatch_call(dest, h2, jnp.zeros((rows, d), f32), tm=tm, nq=nq)
    ys = _moe_ffn_call(block_e, n_used, xs, wg, wu, wd, blk=blk, tf=tf)
    return _combine_call(dest, ys, x, gate, mod, tm=tm, n_lat=n_lat, nq=nq)


def kernel(x, c, ctx, c_ctx, w_ada, b_ada, g_mix, g_ffn, w_in, q_norm, k_norm, w_attn_o, conv_dw, conv_b, conv_ln_g, conv_ln_b, w_conv_o, w_out, w_ff_gate, w_ff_up, w_ff_down, w_router, b_router, w_moe_gate, w_moe_up, w_moe_down, g_final):
    b, s_len, d = x.shape
    n_ctx = ctx.shape[1]
    t = s_len + n_ctx
    depth = w_ada.shape[0]
    tm = 256 if (s_len % 256 == 0 and n_ctx % 256 == 0) else 128
    n_lat = s_len // tm
    nt = t // tm

    cvec = jnp.concatenate([c, c_ctx[None, :]], axis=0)
    mod_all = _ada_call(cvec, w_ada, b_ada)

    cos_t, sin_t = _rope_tables(s_len, t)
    head = lax.broadcasted_iota(jnp.int32, (LANES, LANES), 0) // HEAD_DIM
    seg = (head == head.T).astype(bf16)
    ti = lax.broadcasted_iota(jnp.int32, (tm, tm), 0)
    tri = (ti.T < ti).astype(bf16)
    reps = LANES // HEAD_DIM

    xs = jnp.concatenate([x, ctx], axis=1)
    for l in range(depth):
        last = l == depth - 1
        nq = n_lat if last else nt
        mod = mod_all[l][:, None, :]
        q, kt, v, z, gates = _inproj_call(
            xs, mod, g_mix[l][None, :], w_in[l].astype(bf16),
            jnp.tile(q_norm[l], reps)[None, :], jnp.tile(k_norm[l], reps)[None, :],
            cos_t, sin_t, seg, tm=tm, n_lat=n_lat)
        attn = _attn_call(q, kt, v, tm=tm, n_lat=n_lat, nq=nq, s_len=s_len)
        cz = _conv_call(z, conv_dw[l], conv_b[l][None, :], conv_ln_g[l][None, :], conv_ln_b[l][None, :],
                        tm=tm, n_lat=n_lat, nq=nq)
        moe = l % 2 == 1
        xs, h2 = _merge_call(attn, cz, gates, xs, mod, g_ffn[l][None, :], w_attn_o[l].astype(bf16),
                             w_conv_o[l].astype(bf16), w_out[l].astype(bf16),
                             tm=tm, n_lat=n_lat, nq=nq, h2_dtype=f32 if moe else bf16)
        i = l // 2
        if moe:
            xs = _moe_layer(xs, h2, mod, w_router[i], b_router[i], w_moe_gate[i].astype(bf16),
                            w_moe_up[i].astype(bf16), w_moe_down[i].astype(bf16), tri,
                            tm=tm, n_lat=n_lat, nq=nq)
        else:
            xs = _ffn_call(h2, xs, mod, w_ff_gate[i].astype(bf16), w_ff_up[i].astype(bf16),
                           w_ff_down[i].astype(bf16), tm=tm, n_lat=n_lat, nq=nq)
    return _final_call(xs, g_final[None, :], tm=tm, s_len=s_len)
```
